```python
import math
import jax, jax.numpy as jnp
from jax import lax
import numpy as np

D_MODEL = 2048
BATCH = 2
SEQ = 4096
DEPTH = 2

RET_HEADS = 4
RET_QK_DIM = 64
RET_V_DIM = 128
RET_WIDTH = RET_HEADS * RET_V_DIM
RET_CHUNK = 128
RET_COLS = 2 * RET_HEADS * RET_QK_DIM + 2 * RET_WIDTH

RWKV_HEADS = 8
RWKV_HEAD_DIM = 64
RWKV_WIDTH = RWKV_HEADS * RWKV_HEAD_DIM
RWKV_DECAY_RANK = 96
RWKV_A_RANK = 96
RWKV_GATE_RANK = 256
RWKV_GN_EPS = 64e-5
RWKV_COLS = 3 * RWKV_WIDTH + RWKV_DECAY_RANK + RWKV_A_RANK + RWKV_GATE_RANK

MOBA_HEADS = 8
MOBA_HEAD_DIM = 128
MOBA_WIDTH = MOBA_HEADS * MOBA_HEAD_DIM
MOBA_BLOCK = 256
MOBA_TOPK = 3
MOBA_Q_CHUNK = 32
MOBA_COLS = 3 * MOBA_WIDTH
NEG_INF = -1e30

MIX_WIDTH = RET_WIDTH + RWKV_WIDTH + MOBA_WIDTH
IN_COLS = RET_COLS + RWKV_COLS + MOBA_COLS

D_FF = 5632
CONV_WIDTH = 3
NORM_EPS = 1e-6

kernel_name = "hybrid_retention_rwkv7_moba_convffn_trunk"


def rms_norm(x, g):
    xf = x.astype(jnp.float32)
    y = xf * lax.rsqrt(jnp.mean(xf * xf, axis=-1, keepdims=True) + NORM_EPS)
    return (y * g.astype(jnp.float32)).astype(x.dtype)


def token_shift(x):
    return jnp.pad(x, ((0, 0), (1, 0), (0, 0)))[:, :-1]


def retention_chunkwise(q, k, v):
    B, S, H, Dk = q.shape
    Dv = v.shape[-1]
    C = RET_CHUNK
    N = S // C
    f32 = jnp.float32
    log_g = jnp.log1p(-jnp.exp2(-5.0 - jnp.arange(H, dtype=f32)))

    def chunks(a, d):
        return a.astype(f32).reshape(B, N, C, H, d).transpose(1, 0, 3, 2, 4)

    qc_all = chunks(q, Dk)
    kc_all = chunks(k, Dk) * (Dk ** -0.5)
    vc_all = chunks(v, Dv)
    idx = jnp.arange(C, dtype=f32)
    diff = idx[:, None] - idx[None, :]
    inner_decay = jnp.where(diff >= 0, jnp.exp(log_g[:, None, None] * jnp.maximum(diff, 0.0)), 0.0)
    q_decay = jnp.exp(log_g[:, None] * (idx + 1.0))[..., None]
    k_decay = jnp.exp(log_g[:, None] * (C - 1.0 - idx))[..., None]
    chunk_decay = jnp.exp(log_g * C)[:, None, None]

    def step(R, inp):
        qc, kc, vc = inp
        inner = jnp.einsum('bhid,bhjd->bhij', qc, kc) * inner_decay
        o = jnp.einsum('bhij,bhjv->bhiv', inner, vc) + jnp.einsum('bhid,bhdv->bhiv', qc, R) * q_decay
        R = R * chunk_decay + jnp.einsum('bhjd,bhjv->bhdv', kc * k_decay, vc)
        return R, o

    R0 = jnp.zeros((B, H, Dk, Dv), f32)
    _, o = lax.scan(step, R0, (qc_all, kc_all, vc_all))
    return o.transpose(1, 0, 3, 2, 4).reshape(B, S, H, Dv)


def rwkv7_time_mix(r, k, v, wd, wa, wg, w0, w_up, a0, a_up, g_up, k_k, k_a, r_k, ln_w, ln_b):
    B, S, C = r.shape
    H, N = RWKV_HEADS, RWKV_HEAD_DIM
    f32 = jnp.float32
    r, k, v = r.astype(f32), k.astype(f32), v.astype(f32)
    w_log = -jax.nn.softplus(-(w0 + jnp.tanh(wd.astype(f32)) @ w_up)) - 0.5
    decay = jnp.exp(-jnp.exp(w_log))
    a = jax.nn.sigmoid(a0 + wa.astype(f32) @ a_up)
    g = jax.nn.sigmoid(wg.astype(f32)) @ g_up
    kk = (k * k_k).reshape(B, S, H, N)
    kk = kk / jnp.maximum(jnp.sqrt(jnp.sum(kk * kk, axis=-1, keepdims=True)), 1e-12)
    k = k * (1.0 + (a - 1.0) * k_a)

    def heads_tm(z):
        return jnp.moveaxis(z.reshape(B, S, H, N), 1, 0)

    def step(state, inp):
        r_t, w_t, k_t, v_t, kk_t, a_t = inp
        sa = jnp.einsum('bhij,bhj->bhi', state, -kk_t)
        state = (state * w_t[:, :, None, :]
                 + sa[..., None] * (kk_t * a_t)[:, :, None, :]
                 + v_t[..., None] * k_t[:, :, None, :])
        y = jnp.einsum('bhij,bhj->bhi', state, r_t)
        return state, y

    s0 = jnp.zeros((B, H, N, N), f32)
    _, y = lax.scan(step, s0, (heads_tm(r), heads_tm(decay), heads_tm(k), heads_tm(v),
                               jnp.moveaxis(kk, 1, 0), heads_tm(a)))
    y = jnp.moveaxis(y, 0, 1)
    mean = jnp.mean(y, axis=-1, keepdims=True)
    var = jnp.mean((y - mean) ** 2, axis=-1, keepdims=True)
    y = ((y - mean) * lax.rsqrt(var + RWKV_GN_EPS)).reshape(B, S, C) * ln_w + ln_b
    rh, kh, vh = r.reshape(B, S, H, N), k.reshape(B, S, H, N), v.reshape(B, S, H, N)
    bonus = (jnp.sum(rh * kh * r_k, axis=-1, keepdims=True) * vh).reshape(B, S, C)
    return (y + bonus) * g


def moba_attention(q, k, v):
    B, S, H, D = q.shape
    L = MOBA_BLOCK
    NB = -(-S // L)
    P = NB * L
    topk = min(MOBA_TOPK, NB)
    QC = MOBA_Q_CHUNK
    f32 = jnp.float32
    slopes = jnp.exp2(-8.0 * jnp.arange(1, H + 1, dtype=f32) / H)
    qh = q.astype(f32).transpose(0, 2, 1, 3) * (D ** -0.5)
    pad = ((0, 0), (0, 0), (0, P - S), (0, 0))
    kh = jnp.pad(k.astype(f32).transpose(0, 2, 1, 3), pad)
    vh = jnp.pad(v.astype(f32).transpose(0, 2, 1, 3), pad)
    kb = kh.reshape(B, H, NB, L, D)
    vb = vh.reshape(B, H, NB, L, D)
    k_mean = jnp.mean(kb, axis=3)
    gate = jnp.einsum('bhsd,bhnd->bhsn', qh, k_mean)
    q_block = jnp.arange(S) // L
    fully_past = jnp.arange(NB)[None, :] < q_block[:, None]
    gate = jnp.where(fully_past, gate, -jnp.inf)
    _, sel = lax.top_k(gate, topk)
    b_ix = jnp.arange(B)[:, None, None, None]
    h_ix = jnp.arange(H)[None, :, None, None]
    offs = jnp.arange(L)

    def chunk(ci):
        q0 = ci * QC
        qc = lax.dynamic_slice_in_dim(qh, q0, QC, axis=2)
        sc = lax.dynamic_slice_in_dim(sel, q0, QC, axis=2)
        t = q0 + jnp.arange(QC)
        bq = q0 // L
        kg = kb[b_ix, h_ix, sc]
        vg = vb[b_ix, h_ix, sc]
        s_sel = jnp.einsum('bhqd,bhqnld->bhqnl', qc, kg)
        dist_sel = (t[:, None, None] - (sc[..., None] * L + offs)).astype(f32)
        s_sel = jnp.where((sc < bq)[..., None],
                          s_sel - slopes[:, None, None, None] * dist_sel, NEG_INF)
        ko = lax.dynamic_slice_in_dim(kh, bq * L, L, axis=2)
        vo = lax.dynamic_slice_in_dim(vh, bq * L, L, axis=2)
        s_own = jnp.einsum('bhqd,bhld->bhql', qc, ko)
        dist_own = t[:, None] - (bq * L + offs)[None, :]
        s_own = jnp.where(dist_own >= 0,
                          s_own - slopes[:, None, None] * dist_own.astype(f32), NEG_INF)
        scores = jnp.concatenate([s_sel.reshape(B, H, QC, topk * L), s_own], axis=-1)
        p = jax.nn.softmax(scores, axis=-1)
        p_sel = p[..., :topk * L].reshape(B, H, QC, topk, L)
        p_own = p[..., topk * L:]
        return (jnp.einsum('bhqnl,bhqnld->bhqd', p_sel, vg)
                + jnp.einsum('bhql,bhld->bhqd', p_own, vo))

    out = lax.map(chunk, jnp.arange(S // QC))
    return out.transpose(1, 0, 3, 2, 4).reshape(B, S, H * D)


def hybrid_mixer(h, w_in, w_out, mu, w0, w_up, a0, a_up, g_up, k_k, k_a, r_k, ln_w, ln_b):
    B, S, _ = h.shape
    proj = h @ w_in
    o0 = 0
    qk = RET_HEADS * RET_QK_DIM
    rq = proj[..., o0:o0 + qk].reshape(B, S, RET_HEADS, RET_QK_DIM)
    rk = proj[..., o0 + qk:o0 + 2 * qk].reshape(B, S, RET_HEADS, RET_QK_DIM)
    rv = proj[..., o0 + 2 * qk:o0 + 2 * qk + RET_WIDTH].reshape(B, S, RET_HEADS, RET_V_DIM)
    rg = proj[..., o0 + 2 * qk + RET_WIDTH:o0 + RET_COLS].astype(jnp.float32)
    ro = retention_chunkwise(rq, rk, rv)
    ro = ro * lax.rsqrt(jnp.mean(ro * ro, axis=-1, keepdims=True) + NORM_EPS)
    ret_out = ro.reshape(B, S, RET_WIDTH) * jax.nn.silu(rg)
    rw = proj[..., RET_COLS:RET_COLS + RWKV_COLS]
    rw = rw + (token_shift(rw) - rw) * mu
    c1 = RWKV_WIDTH
    c2 = 2 * RWKV_WIDTH
    c3 = 3 * RWKV_WIDTH
    c4 = c3 + RWKV_DECAY_RANK
    c5 = c4 + RWKV_A_RANK
    rwkv_out = rwkv7_time_mix(rw[..., :c1], rw[..., c1:c2], rw[..., c2:c3],
                              rw[..., c3:c4], rw[..., c4:c5], rw[..., c5:],
                              w0, w_up, a0, a_up, g_up, k_k, k_a, r_k, ln_w, ln_b)
    m0 = RET_COLS + RWKV_COLS
    mq = proj[..., m0:m0 + MOBA_WIDTH].reshape(B, S, MOBA_HEADS, MOBA_HEAD_DIM)
    mk = proj[..., m0 + MOBA_WIDTH:m0 + 2 * MOBA_WIDTH].reshape(B, S, MOBA_HEADS, MOBA_HEAD_DIM)
    mv = proj[..., m0 + 2 * MOBA_WIDTH:m0 + 3 * MOBA_WIDTH].reshape(B, S, MOBA_HEADS, MOBA_HEAD_DIM)
    moba_out = moba_attention(mq, mk, mv)
    mixed = jnp.concatenate([ret_out.astype(h.dtype), rwkv_out.astype(h.dtype),
                             moba_out.astype(h.dtype)], axis=-1)
    return mixed @ w_out


def conv_glu_ffn(h, w_up, conv_w, conv_b, w_down):
    S = h.shape[1]
    u = h @ w_up
    up = jnp.pad(u, ((0, 0), (CONV_WIDTH - 1, 0), (0, 0)))
    u = sum(conv_w[j] * up[:, j:j + S] for j in range(CONV_WIDTH)) + conv_b
    val, gate = jnp.split(u, 2, axis=-1)
    return (jax.nn.silu(gate) * val) @ w_down


def setup_inputs(seed: int = 0) -> dict:
    key = jax.random.key(seed)
    ks = jax.random.split(key, 24)
    D, L = D_MODEL, DEPTH
    f32 = jnp.float32

    def nrm(k, shape, scale):
        return jax.random.normal(k, shape, f32) * scale

    ratio = (jnp.arange(RWKV_WIDTH, dtype=f32) / (RWKV_WIDTH - 1)) ** 0.85
    return {
        "x": nrm(ks[0], (BATCH, SEQ, D), 1.0),
        "c": nrm(ks[1], (BATCH, D), 1.0),
        "w_in": nrm(ks[2], (L, D, IN_COLS), D ** -0.5),
        "w_out": nrm(ks[3], (L, MIX_WIDTH, D), MIX_WIDTH ** -0.5),
        "rwkv_mu": jax.random.uniform(ks[4], (L, RWKV_COLS), f32),
        "rwkv_w0": (-6.5 + 5.0 * ratio)[None, :] + nrm(ks[5], (L, RWKV_WIDTH), 0.1),
        "rwkv_w_up": nrm(ks[6], (L, RWKV_DECAY_RANK, RWKV_WIDTH), 0.1 * RWKV_DECAY_RANK ** -0.5),
        "rwkv_a0": nrm(ks[7], (L, RWKV_WIDTH), 0.1),
        "rwkv_a_up": nrm(ks[8], (L, RWKV_A_RANK, RWKV_WIDTH), RWKV_A_RANK ** -0.5),
        "rwkv_g_up": nrm(ks[9], (L, RWKV_GATE_RANK, RWKV_WIDTH), RWKV_GATE_RANK ** -0.5),
        "rwkv_k_k": 0.85 + nrm(ks[10], (L, RWKV_WIDTH), 0.05),
        "rwkv_k_a": 1.0 + nrm(ks[11], (L, RWKV_WIDTH), 0.05),
        "rwkv_r_k": -0.04 + nrm(ks[12], (L, RWKV_HEADS, RWKV_HEAD_DIM), 0.1),
        "rwkv_ln_w": 1.0 + nrm(ks[13], (L, RWKV_WIDTH), 0.05),
        "rwkv_ln_b": nrm(ks[14], (L, RWKV_WIDTH), 0.02),
        "w_ffn_up": nrm(ks[15], (L, D, 2 * D_FF), D ** -0.5),
        "ffn_conv_w": nrm(ks[16], (L, CONV_WIDTH, 2 * D_FF), CONV_WIDTH ** -0.5),
        "ffn_conv_b": nrm(ks[17], (L, 2 * D_FF), 0.02),
        "w_ffn_down": nrm(ks[18], (L, D_FF, D), D_FF ** -0.5),
        "w_ada": nrm(ks[19], (L, D, 6 * D), 0.5 * D ** -0.5),
        "b_ada": nrm(ks[20], (L, 6 * D), 0.02),
        "norm_mix": 1.0 + nrm(ks[21], (L, D), 0.05),
        "norm_ffn": 1.0 + nrm(ks[22], (L, D), 0.05),
        "norm_final": 1.0 + nrm(ks[23], (D,), 0.05),
    }


def reference(x, c, w_in, w_out, rwkv_mu, rwkv_w0, rwkv_w_up, rwkv_a0, rwkv_a_up, rwkv_g_up,
              rwkv_k_k, rwkv_k_a, rwkv_r_k, rwkv_ln_w, rwkv_ln_b, w_ffn_up, ffn_conv_w,
              ffn_conv_b, w_ffn_down, w_ada, b_ada, norm_mix, norm_ffn, norm_final):
    c_act = jax.nn.silu(c)
    for l in range(DEPTH):
        mod = (c_act @ w_ada[l] + b_ada[l])[:, None, :]
        sh1, sc1, g1, sh2, sc2, g2 = jnp.split(mod, 6, axis=-1)
        h = rms_norm(x, norm_mix[l]) * (1.0 + sc1) + sh1
        x = x + g1 * hybrid_mixer(h, w_in[l], w_out[l], rwkv_mu[l], rwkv_w0[l], rwkv_w_up[l],
                                  rwkv_a0[l], rwkv_a_up[l], rwkv_g_up[l], rwkv_k_k[l],
                                  rwkv_k_a[l], rwkv_r_k[l], rwkv_ln_w[l], rwkv_ln_b[l])
        h = rms_norm(x, norm_ffn[l]) * (1.0 + sc2) + sh2
        x = x + g2 * conv_glu_ffn(h, w_ffn_up[l], ffn_conv_w[l], ffn_conv_b[l], w_ffn_down[l])
    return rms_norm(x, norm_final)
```

```python
import functools
import math

import jax
import jax.numpy as jnp
from jax import lax
from jax.experimental import pallas as pl
from jax.experimental.pallas import tpu as pltpu

F32 = jnp.float32
BF16 = jnp.bfloat16
LANES = 128

RET_HEADS = 4
RET_QK = 64
RET_V = 128
RET_WIDTH = RET_HEADS * RET_V
RET_COLS = 2 * RET_HEADS * RET_QK + 2 * RET_WIDTH
RET_COLS_P = 4 * RET_WIDTH

RWKV_HEADS = 8
RWKV_N = 64
RWKV_WIDTH = RWKV_HEADS * RWKV_N
RWKV_DECAY_RANK = 96
RWKV_A_RANK = 96
RWKV_GATE_RANK = 256
RWKV_GN_EPS = 64e-5
RWKV_RANK_PAD = LANES
RWKV_COLS = 3 * RWKV_WIDTH + RWKV_DECAY_RANK + RWKV_A_RANK + RWKV_GATE_RANK
RWKV_COLS_P = 3 * RWKV_WIDTH + 2 * RWKV_RANK_PAD + RWKV_GATE_RANK

MOBA_HEADS = 8
MOBA_D = 128
MOBA_WIDTH = MOBA_HEADS * MOBA_D
MOBA_BLOCK = 256
MOBA_TOPK = 3
MOBA_COLS = 3 * MOBA_WIDTH
NEG_INF = -1e30

IN_COLS_P = RET_COLS_P + RWKV_COLS_P + MOBA_COLS
RWKV_OFF = RET_COLS_P
MOBA_OFF = RET_COLS_P + RWKV_COLS_P

CONV_WIDTH = 3
NORM_EPS = 1e-6

VMEM_LIMIT_BYTES = 56 * 1024 * 1024

RWKV_CHUNK = 64
RWKV_GROUP = 256


def _cparams(sem):
    return pltpu.CompilerParams(dimension_semantics=sem, vmem_limit_bytes=VMEM_LIMIT_BYTES)


def _dot(a, b):
    return jnp.dot(a, b, preferred_element_type=F32)


def _dot_nt(a, b, precision=None):
    return lax.dot_general(a, b, (((1,), (1,)), ((), ())), preferred_element_type=F32,
                           precision=precision)


def _dot_tn(a, b):
    return lax.dot_general(a, b, (((0,), (0,)), ((), ())), preferred_element_type=F32)


def _sigmoid(x):
    return 1.0 / (1.0 + jnp.exp(-x))


def _silu(x):
    return x * _sigmoid(x)


def _rms_mod(x, nw, sc, sh):
    ms = jnp.mean(x * x, axis=-1, keepdims=True)
    y = x * lax.rsqrt(ms + NORM_EPS) * nw
    return y * (1.0 + sc) + sh


def _ada_kernel(c_ref, w_ref, b_ref, o_ref):
    o_ref[...] = _dot(_silu(c_ref[...]), w_ref[...]) + b_ref[...]


def _ada_mod(c, w_ada, b_ada):
    nl, d, n = w_ada.shape
    b = c.shape[0]
    tn = 1024
    return pl.pallas_call(
        _ada_kernel,
        grid=(nl, n // tn),
        in_specs=[pl.BlockSpec((b, d), lambda l, j: (0, 0)),
                  pl.BlockSpec((None, d, tn), lambda l, j: (l, 0, j)),
                  pl.BlockSpec((None, 1, tn), lambda l, j: (l, 0, j))],
        out_specs=pl.BlockSpec((None, b, tn), lambda l, j: (l, 0, j)),
        out_shape=jax.ShapeDtypeStruct((nl, b, n), F32),
        compiler_params=_cparams(("arbitrary", "arbitrary")),
        name="ada_mod",
    )(c, w_ada, b_ada.reshape(nl, 1, n))


def _mod_spec(l, k, d, tiles_per_batch):
    return pl.BlockSpec((None, None, None, 1, d),
                        lambda i, *_: (l, i // tiles_per_batch, k, 0, 0))


def _norm_mod_kernel(x_ref, nw_ref, sc_ref, sh_ref, h_ref):
    h_ref[...] = _rms_mod(x_ref[...], nw_ref[...], sc_ref[...], sh_ref[...]).astype(h_ref.dtype)


def _norm_mod(xf, nw, mod5, l, seq):
    m, d = xf.shape
    tm = 512
    tpb = seq // tm
    return pl.pallas_call(
        _norm_mod_kernel,
        grid=(m // tm,),
        in_specs=[pl.BlockSpec((tm, d), lambda i: (i, 0)),
                  pl.BlockSpec((1, d), lambda i: (0, 0)),
                  _mod_spec(l, 1, d, tpb), _mod_spec(l, 0, d, tpb)],
        out_specs=pl.BlockSpec((tm, d), lambda i: (i, 0)),
        out_shape=jax.ShapeDtypeStruct((m, d), BF16),
        compiler_params=_cparams(("arbitrary",)),
        name="norm_mod",
    )(xf, nw.reshape(1, d), mod5, mod5)


def _mm_kernel(a_ref, w_ref, o_ref):
    o_ref[...] = _dot(a_ref[...], w_ref[...]).astype(o_ref.dtype)


def _in_proj(h, w):
    m, d = h.shape
    n = w.shape[1]
    tm, tn = 1024, 512
    return pl.pallas_call(
        _mm_kernel,
        grid=(m // tm, n // tn),
        in_specs=[pl.BlockSpec((tm, d), lambda i, j: (i, 0)),
                  pl.BlockSpec((d, tn), lambda i, j: (0, j))],
        out_specs=pl.BlockSpec((tm, tn), lambda i, j: (i, j)),
        out_shape=jax.ShapeDtypeStruct((m, n), F32),
        compiler_params=_cparams(("arbitrary", "arbitrary")),
        name="in_proj",
    )(h, w)


def _ret_kernel(x_ref, o_ref, r_ref, *, chunk):
    c = chunk
    wv = RET_WIDTH

    @pl.when(pl.program_id(1) == 0)
    def _():
        r_ref[...] = jnp.zeros_like(r_ref)

    ti = lax.broadcasted_iota(jnp.int32, (c, c), 0)
    tj = lax.broadcasted_iota(jnp.int32, (c, c), 1)
    diff = (ti - tj).astype(F32)
    tcol = lax.broadcasted_iota(jnp.int32, (c, 1), 0).astype(F32)
    for h in range(RET_HEADS):
        lg = math.log1p(-(2.0 ** (-5 - h)))
        hs = slice(h * RET_V, (h + 1) * RET_V)
        q = x_ref[:, hs]
        k = x_ref[:, wv + h * RET_V:wv + (h + 1) * RET_V] * (RET_QK ** -0.5)
        v = x_ref[:, 2 * wv + h * RET_V:2 * wv + (h + 1) * RET_V]
        g = x_ref[:, 3 * wv + h * RET_V:3 * wv + (h + 1) * RET_V]
        dec = jnp.where(diff >= 0, jnp.exp(lg * jnp.maximum(diff, 0.0)), 0.0)
        qb = q.astype(BF16)
        vb = v.astype(BF16)
        s = _dot_nt(qb, k.astype(BF16)) * dec
        r_old = r_ref[h]
        q_decay = jnp.exp(lg * (tcol + 1.0))
        k_decay = jnp.exp(lg * (c - 1.0 - tcol))
        o = _dot(s.astype(BF16), vb) + _dot(qb, r_old.astype(BF16)) * q_decay
        r_ref[h] = r_old * math.exp(lg * c) + _dot_tn((k * k_decay).astype(BF16), vb)
        ro = o * lax.rsqrt(jnp.mean(o * o, axis=-1, keepdims=True) + NORM_EPS)
        o_ref[:, hs] = (ro * _silu(g)).astype(o_ref.dtype)


def _retention(proj, batch, seq):
    m = proj.shape[0]
    c = 256
    nc = seq // c
    return pl.pallas_call(
        functools.partial(_ret_kernel, chunk=c),
        grid=(batch, nc),
        in_specs=[pl.BlockSpec((c, RET_COLS_P), lambda b, i: (b * nc + i, 0))],
        out_specs=pl.BlockSpec((c, RET_WIDTH), lambda b, i: (b * nc + i, 0)),
        out_shape=jax.ShapeDtypeStruct((m, RET_WIDTH), BF16),
        scratch_shapes=[pltpu.VMEM((RET_HEADS, RET_V, RET_V), F32)],
        compiler_params=_cparams(("arbitrary", "arbitrary")),
        name="retention",
    )(proj)


def _split_bf16(x):
    hi = x.astype(BF16)
    return hi, (x - hi.astype(F32)).astype(BF16)


def _seg_sum(x, seg_ones):
    hi, lo = _split_bf16(x)
    return _dot(hi, seg_ones) + _dot(lo, seg_ones)


def _block_diag(x, mask):
    reps = mask.shape[0] // x.shape[0]
    tiled = jnp.concatenate([x] * reps, axis=0)
    return jnp.where(mask, tiled, 0.0).astype(BF16)


def _rwkv_kernel(x_ref, mu_ref, w0_ref, wup_ref, a0_ref, aup_ref, gup_ref, kk_ref, ka_ref,
                 rk_ref, lnw_ref, lnb_ref, o_ref, xs_ref, s_ref, *, tile):
    t = tile
    c = RWKV_CHUNK
    w = RWKV_WIDTH
    gw = RWKV_GROUP
    n = RWKV_N

    @pl.when(pl.program_id(1) == 0)
    def _():
        xs_ref[0:8, :] = jnp.zeros((8, xs_ref.shape[1]), F32)
        s_ref[...] = jnp.zeros_like(s_ref)

    x = x_ref[...]
    xs_ref[8:t + 8, :] = x
    xp = xs_ref[7:t + 7, :]
    xs_ref[0:8, :] = x[t - 8:t, :]
    rw = x + (xp - x) * mu_ref[...]

    r = rw[:, 0:w]
    k = rw[:, w:2 * w]
    v = rw[:, 2 * w:3 * w]
    o1 = 3 * w
    wd = rw[:, o1:o1 + RWKV_RANK_PAD]
    wa = rw[:, o1 + RWKV_RANK_PAD:o1 + 2 * RWKV_RANK_PAD]
    wg = rw[:, o1 + 2 * RWKV_RANK_PAD:o1 + 2 * RWKV_RANK_PAD + RWKV_GATE_RANK]

    u = -(w0_ref[...] + _dot(jnp.tanh(wd).astype(BF16), wup_ref[...]))
    softplus = jnp.maximum(u, 0.0) + jnp.log(1.0 + jnp.exp(-jnp.abs(u)))
    lw = -jnp.exp(-softplus - 0.5)
    a = _sigmoid(a0_ref[...] + _dot(wa.astype(BF16), aup_ref[...]))
    g = _dot(_sigmoid(wg).astype(BF16), gup_ref[...])

    seg_r = lax.broadcasted_iota(jnp.int32, (w, w), 0) // n
    seg_c = lax.broadcasted_iota(jnp.int32, (w, w), 1) // n
    seg_ones = jnp.where(seg_r == seg_c, 1.0, 0.0).astype(BF16)

    kk = k * kk_ref[...]
    kkn = kk / jnp.maximum(jnp.sqrt(_seg_sum(kk * kk, seg_ones)), 1e-12)
    k2 = k * (1.0 + (a - 1.0) * ka_ref[...])
    beta = kkn * a

    ci = lax.broadcasted_iota(jnp.int32, (c, c), 0)
    cj = lax.broadcasted_iota(jnp.int32, (c, c), 1)
    tri_incl = jnp.where(cj <= ci, 1.0, 0.0).astype(BF16)
    row_t = lax.broadcasted_iota(jnp.int32, (c, gw), 0)
    lane_s = lax.broadcasted_iota(jnp.int32, (c, gw), 1) % n
    strict = lane_s < row_t
    lower = lane_s <= row_t
    eye_cat = jnp.where(lane_s == row_t, 1.0, 0.0)
    bd_r = lax.broadcasted_iota(jnp.int32, (gw, gw), 0) // n
    bd_c = lax.broadcasted_iota(jnp.int32, (gw, gw), 1) // n
    bd_mask = bd_r == bd_c

    y_chunks = []
    for ch in range(t // c):
        rows = slice(ch * c, (ch + 1) * c)
        lw_c = lw[rows]
        lw_hi, lw_lo = _split_bf16(lw_c)
        cum = _dot(tri_incl, lw_hi) + _dot(tri_incl, lw_lo)
        ctot = cum[c - 1:c, :]
        g_inv = jnp.exp(-cum)
        g_end = jnp.exp(ctot - cum)
        g_tot = jnp.exp(ctot)
        al = -kkn[rows] * jnp.exp(cum - lw_c)
        bt = beta[rows] * g_inv
        kt = k2[rows] * g_inv
        rt = r[rows] * jnp.exp(cum)
        bh = beta[rows] * g_end
        kh = k2[rows] * g_end
        v_c = v[rows]
        y_groups = []
        for gi in range(w // gw):
            ls = slice(gi * gw, (gi + 1) * gw)
            al_g = al[:, ls].astype(BF16)
            rt_g = rt[:, ls].astype(BF16)
            v_bd = _block_diag(v_c[:, ls], bd_mask)
            gram = _dot_nt(jnp.concatenate([al_g, rt_g], axis=0),
                           jnp.concatenate([_block_diag(bt[:, ls], bd_mask),
                                            _block_diag(kt[:, ls], bd_mask)], axis=0))
            a_ab = jnp.where(strict, gram[0:c, 0:gw], 0.0)
            a_ak = jnp.where(strict, gram[0:c, gw:2 * gw], 0.0)
            a_rb = jnp.where(lower, gram[c:2 * c, 0:gw], 0.0)
            a_rk = jnp.where(lower, gram[c:2 * c, gw:2 * gw], 0.0)
            xk = a_ab
            inv = eye_cat + a_ab
            for _ in range(int(math.log2(c)) - 1):
                xk = _dot(xk.astype(BF16), _block_diag(xk, bd_mask))
                inv = inv + _dot(inv.astype(BF16), _block_diag(xk, bd_mask))
            s_old = s_ref[gi]
            s_b = s_old.astype(BF16)
            wmat = _dot_nt(al_g, s_b) + _dot(a_ak.astype(BF16), v_bd)
            umat = _dot(inv.astype(BF16), _block_diag(wmat, bd_mask))
            y_g = _dot_nt(rt_g, s_b) + _dot(
                jnp.concatenate([a_rb, a_rk], axis=1).astype(BF16),
                jnp.concatenate([_block_diag(umat, bd_mask), v_bd], axis=0))
            upd = _dot_tn(jnp.concatenate([umat, v_c[:, ls]], axis=0).astype(BF16),
                          jnp.concatenate([bh[:, ls], kh[:, ls]], axis=0).astype(BF16))
            s_ref[gi] = s_old * g_tot[:, ls] + jnp.where(bd_mask, upd, 0.0)
            y_groups.append(y_g)
        y_chunks.append(jnp.concatenate(y_groups, axis=1))
    y = jnp.concatenate(y_chunks, axis=0)

    mean = _seg_sum(y, seg_ones) * (1.0 / n)
    yc = y - mean
    var = _seg_sum(yc * yc, seg_ones) * (1.0 / n)
    yn = yc * lax.rsqrt(var + RWKV_GN_EPS) * lnw_ref[...] + lnb_ref[...]
    bonus = _seg_sum(r * k2 * rk_ref[...], seg_ones) * v
    o_ref[...] = ((yn + bonus) * g).astype(o_ref.dtype)


def _rwkv(proj, batch, seq, mu, w0, wup, a0, aup, gup, k_k, k_a, r_k, ln_w, ln_b):
    m = proj.shape[0]
    t = 256
    nt = seq // t
    w = RWKV_WIDTH
    sec = RWKV_OFF // RWKV_COLS_P
    vec = lambda width: pl.BlockSpec((1, width), lambda b, i: (0, 0))
    mat = lambda rows, cols: pl.BlockSpec((rows, cols), lambda b, i: (0, 0))
    return pl.pallas_call(
        functools.partial(_rwkv_kernel, tile=t),
        grid=(batch, nt),
        in_specs=[pl.BlockSpec((t, RWKV_COLS_P), lambda b, i: (b * nt + i, sec)),
                  vec(RWKV_COLS_P), vec(w), mat(RWKV_RANK_PAD, w), vec(w), mat(RWKV_RANK_PAD, w),
                  mat(RWKV_GATE_RANK, w), vec(w), vec(w), vec(w), vec(w), vec(w)],
        out_specs=pl.BlockSpec((t, w), lambda b, i: (b * nt + i, 0)),
        out_shape=jax.ShapeDtypeStruct((m, w), BF16),
        scratch_shapes=[pltpu.VMEM((t + 8, RWKV_COLS_P), F32),
                        pltpu.VMEM((w // RWKV_GROUP, RWKV_GROUP, RWKV_GROUP), F32)],
        compiler_params=_cparams(("arbitrary", "arbitrary")),
        name="rwkv7",
    )(proj, mu, w0, wup, a0, aup, gup, k_k, k_a, r_k, ln_w, ln_b)


def _moba_kernel(q_ref, k_ref, v_ref, o_ref, km_ref, *, nblocks):
    blk = MOBA_BLOCK
    d = MOBA_D
    h = pl.program_id(1)
    bq = pl.program_id(2)

    @pl.when(bq == 0)
    def _():
        km_ref[...] = jnp.zeros_like(km_ref)

    slope = jnp.exp2(jnp.full((1, blk), -8.0 / MOBA_HEADS, F32) * (h + 1).astype(F32))
    q = q_ref[...] * (d ** -0.5)
    qb = q.astype(BF16)

    lane = lax.broadcasted_iota(jnp.int32, (blk, LANES), 1)
    valid = lane < bq
    gate = _dot_nt(q, km_ref[...], precision=lax.Precision.HIGHEST)
    gm = jnp.where(valid, gate, -jnp.inf)
    rank = jnp.zeros((blk, LANES), F32)
    for n2 in range(nblocks):
        col = gm[:, n2:n2 + 1]
        tie = jnp.where(lane > n2, 1.0, 0.0)
        rank = rank + jnp.where(col > gm, 1.0, jnp.where(col == gm, tie, 0.0))
    pen = jnp.where(valid, jnp.where(rank < float(MOBA_TOPK), 0.0, NEG_INF), NEG_INF)
    qa = jnp.concatenate([qb, pen.astype(BF16)], axis=1)

    ri = lax.broadcasted_iota(jnp.int32, (blk, blk), 0)
    rj = lax.broadcasted_iota(jnp.int32, (blk, blk), 1)
    rel = (ri - rj).astype(F32)
    ones = jnp.ones((blk, LANES), BF16)

    row0 = pl.multiple_of(bq * blk, blk)
    k_own = k_ref[pl.ds(row0, blk), :]
    v_own = v_ref[pl.ds(row0, blk), :]
    s = _dot_nt(qb, k_own.astype(BF16))
    s = jnp.where(rel >= 0, s - slope * rel, NEG_INF)
    m0 = jnp.max(s, axis=1, keepdims=True)
    p = jnp.exp(s - m0)
    acc0 = _dot(p.astype(BF16), jnp.concatenate([v_own.astype(BF16), ones], axis=1))

    def body(nb, carry):
        m_run, acc = carry
        r0 = pl.multiple_of(nb * blk, blk)
        kn = k_ref[pl.ds(r0, blk), :].astype(BF16)
        vn = v_ref[pl.ds(r0, blk), :].astype(BF16)
        ka = jnp.concatenate([kn, jnp.where(lane == nb, 1.0, 0.0).astype(BF16)], axis=1)
        sc = _dot_nt(qa, ka) - slope * (rel + ((bq - nb) * blk).astype(F32))
        m_new = jnp.maximum(m_run, jnp.max(sc, axis=1, keepdims=True))
        pn = jnp.exp(sc - m_new)
        acc = acc * jnp.exp(m_run - m_new) + _dot(pn.astype(BF16),
                                                  jnp.concatenate([vn, ones], axis=1))
        return m_new, acc

    _, acc = lax.fori_loop(0, bq, body, (m0, acc0))
    o_ref[...] = (acc[:, 0:d] / acc[:, d:2 * d]).astype(o_ref.dtype)

    kmean = jnp.mean(k_own, axis=0, keepdims=True)
    krow = lax.broadcasted_iota(jnp.int32, km_ref.shape, 0)
    km_ref[...] = jnp.where(krow == bq, kmean, km_ref[...])


def _moba(proj, batch, seq):
    m = proj.shape[0]
    blk = MOBA_BLOCK
    nb = seq // blk
    d = MOBA_D
    qc = MOBA_OFF // d
    kc = (MOBA_OFF + MOBA_WIDTH) // d
    vc = (MOBA_OFF + 2 * MOBA_WIDTH) // d
    return pl.pallas_call(
        functools.partial(_moba_kernel, nblocks=nb),
        grid=(batch, MOBA_HEADS, nb),
        in_specs=[pl.BlockSpec((blk, d), lambda b, h, i: (b * nb + i, qc + h)),
                  pl.BlockSpec((seq, d), lambda b, h, i: (b, kc + h)),
                  pl.BlockSpec((seq, d), lambda b, h, i: (b, vc + h))],
        out_specs=pl.BlockSpec((blk, d), lambda b, h, i: (b * nb + i, h)),
        out_shape=jax.ShapeDtypeStruct((m, MOBA_WIDTH), BF16),
        scratch_shapes=[pltpu.VMEM((LANES, d), F32)],
        compiler_params=_cparams(("arbitrary", "arbitrary", "arbitrary")),
        name="moba",
    )(proj, proj, proj)


def _out_proj_kernel(ret_ref, rw_ref, mo_ref, w_ref, x_ref, g_ref, nw_ref, sc_ref, sh_ref,
                     xo_ref, h_ref):
    mixed = jnp.concatenate([ret_ref[...], rw_ref[...], mo_ref[...]], axis=1)
    xn = x_ref[...] + g_ref[...] * _dot(mixed, w_ref[...])
    xo_ref[...] = xn
    h_ref[...] = _rms_mod(xn, nw_ref[...], sc_ref[...], sh_ref[...]).astype(h_ref.dtype)


def _out_proj(ret, rwo, mo, w, xf, mod5, l, nw, seq):
    m, d = xf.shape
    tm = 256
    tpb = seq // tm
    row = lambda width: pl.BlockSpec((tm, width), lambda i: (i, 0))
    return pl.pallas_call(
        _out_proj_kernel,
        grid=(m // tm,),
        in_specs=[row(RET_WIDTH), row(RWKV_WIDTH), row(MOBA_WIDTH),
                  pl.BlockSpec((d, d), lambda i: (0, 0)),
                  row(d), _mod_spec(l, 2, d, tpb),
                  pl.BlockSpec((1, d), lambda i: (0, 0)),
                  _mod_spec(l, 4, d, tpb), _mod_spec(l, 3, d, tpb)],
        out_specs=[row(d), row(d)],
        out_shape=[jax.ShapeDtypeStruct((m, d), F32), jax.ShapeDtypeStruct((m, d), BF16)],
        compiler_params=_cparams(("arbitrary",)),
        name="out_proj",
    )(ret, rwo, mo, w, xf, mod5, nw.reshape(1, d), mod5, mod5)


FFN_HALO = 16


def _ffn_up_kernel(h_ref, halo_ref, wv_ref, wg_ref, cwv_ref, cwg_ref, cbv_ref, cbg_ref, o_ref,
                   u_ref, *, tile, tiles_per_batch):
    tm = tile
    hl = FFN_HALO
    first = (pl.program_id(0) % tiles_per_batch) == 0
    halo = jnp.where(first, jnp.zeros_like(halo_ref), halo_ref[...])
    lhs = jnp.concatenate([halo, h_ref[...]], axis=0)

    def branch(w_ref, cw_ref, cb_ref):
        u_ref[...] = _dot(lhs, w_ref[...])
        cw = cw_ref[...]
        return (cw[0:1, :] * u_ref[hl - 2:hl - 2 + tm, :] + cw[1:2, :] * u_ref[hl - 1:hl - 1 + tm, :]
                + cw[2:3, :] * u_ref[hl:hl + tm, :] + cb_ref[...])

    val = branch(wv_ref, cwv_ref, cbv_ref)
    gate = branch(wg_ref, cwg_ref, cbg_ref)
    o_ref[...] = (_silu(gate) * val).astype(o_ref.dtype)


def _ffn_up(h2, w, cw, cb, seq):
    m, d = h2.shape
    dff = w.shape[1] // 2
    tm, tn = 1024, 512
    hl = FFN_HALO
    tpb = seq // tm
    nj = dff // tn
    cb2 = cb.reshape(1, -1)
    return pl.pallas_call(
        functools.partial(_ffn_up_kernel, tile=tm, tiles_per_batch=tpb),
        grid=(m // tm, nj),
        in_specs=[pl.BlockSpec((tm, d), lambda i, j: (i, 0)),
                  pl.BlockSpec((hl, d), lambda i, j: (jnp.maximum(i * (tm // hl) - 1, 0), 0)),
                  pl.BlockSpec((d, tn), lambda i, j: (0, j)),
                  pl.BlockSpec((d, tn), lambda i, j: (0, nj + j)),
                  pl.BlockSpec((CONV_WIDTH, tn), lambda i, j: (0, j)),
                  pl.BlockSpec((CONV_WIDTH, tn), lambda i, j: (0, nj + j)),
                  pl.BlockSpec((1, tn), lambda i, j: (0, j)),
                  pl.BlockSpec((1, tn), lambda i, j: (0, nj + j))],
        out_specs=pl.BlockSpec((tm, tn), lambda i, j: (i, j)),
        out_shape=jax.ShapeDtypeStruct((m, dff), BF16),
        scratch_shapes=[pltpu.VMEM((hl + tm, tn), F32)],
        compiler_params=_cparams(("arbitrary", "arbitrary")),
        name="ffn_up",
    )(h2, h2, w, w, cw, cw, cb2, cb2)


def _ffn_down_kernel(a_ref, w_ref, x_ref, g_ref, nw_ref, *rest, nk, final):
    if final:
        o_ref, acc_ref = rest
    else:
        sc_ref, sh_ref, xo_ref, h_ref, acc_ref = rest
    kstep = pl.program_id(1)

    @pl.when(kstep == 0)
    def _():
        acc_ref[...] = jnp.zeros_like(acc_ref)

    acc_ref[...] += _dot(a_ref[...], w_ref[...])

    @pl.when(kstep == nk - 1)
    def _():
        xn = x_ref[...] + g_ref[...] * acc_ref[...]
        if final:
            ms = jnp.mean(xn * xn, axis=-1, keepdims=True)
            o_ref[...] = xn * lax.rsqrt(ms + NORM_EPS) * nw_ref[...]
        else:
            xo_ref[...] = xn
            h_ref[...] = _rms_mod(xn, nw_ref[...], sc_ref[...], sh_ref[...]).astype(h_ref.dtype)


def _ffn_down(act, w, xf, mod5, l, nw, seq, final):
    m, d = xf.shape
    dff = act.shape[1]
    tm = 512
    nk = 4
    tk = dff // nk
    tpb = seq // tm
    row = pl.BlockSpec((tm, d), lambda i, k: (i, 0))
    in_specs = [pl.BlockSpec((tm, tk), lambda i, k: (i, k)),
                pl.BlockSpec((tk, d), lambda i, k: (k, 0)),
                row, _mod_spec(l, 5, d, tpb),
                pl.BlockSpec((1, d), lambda i, k: (0, 0))]
    args = [act, w, xf, mod5, nw.reshape(1, d)]
    if final:
        out_specs = row
        out_shape = jax.ShapeDtypeStruct((m, d), F32)
    else:
        in_specs += [_mod_spec(l + 1, 1, d, tpb), _mod_spec(l + 1, 0, d, tpb)]
        args += [mod5, mod5]
        out_specs = [row, row]
        out_shape = [jax.ShapeDtypeStruct((m, d), F32), jax.ShapeDtypeStruct((m, d), BF16)]
    return pl.pallas_call(
        functools.partial(_ffn_down_kernel, nk=nk, final=final),
        grid=(m // tm, nk),
        in_specs=in_specs,
        out_specs=out_specs,
        out_shape=out_shape,
        scratch_shapes=[pltpu.VMEM((tm, d), F32)],
        compiler_params=_cparams(("arbitrary", "arbitrary")),
        name="ffn_down_final" if final else "ffn_down",
    )(*args)


def _pad_ret_heads(a):
    lead = a.shape[:-1]
    a = a.reshape(lead + (RET_HEADS, RET_QK))
    a = jnp.pad(a, [(0, 0)] * len(lead) + [(0, 0), (0, RET_V - RET_QK)])
    return a.reshape(lead + (RET_HEADS * RET_V,))


def _pad_rwkv_codes(a):
    o = 3 * RWKV_WIDTH
    z = jnp.zeros(a.shape[:-1] + (RWKV_RANK_PAD - RWKV_DECAY_RANK,), a.dtype)
    return jnp.concatenate([a[..., :o + RWKV_DECAY_RANK], z,
                            a[..., o + RWKV_DECAY_RANK:o + RWKV_DECAY_RANK + RWKV_A_RANK], z,
                            a[..., o + RWKV_DECAY_RANK + RWKV_A_RANK:]], axis=-1)


def _prep_w_in(w):
    nqk = RET_HEADS * RET_QK
    ret = jnp.concatenate([_pad_ret_heads(w[:, 0:nqk]), _pad_ret_heads(w[:, nqk:2 * nqk]),
                           w[:, 2 * nqk:RET_COLS]], axis=1)
    rwkv = _pad_rwkv_codes(w[:, RET_COLS:RET_COLS + RWKV_COLS])
    return jnp.concatenate([ret, rwkv, w[:, RET_COLS + RWKV_COLS:]], axis=1).astype(BF16)


def _pad_rows(a, rows):
    return jnp.pad(a, ((0, rows - a.shape[0]), (0, 0)))


def kernel(x, c, w_in, w_out, rwkv_mu, rwkv_w0, rwkv_w_up, rwkv_a0, rwkv_a_up, rwkv_g_up, rwkv_k_k, rwkv_k_a, rwkv_r_k, rwkv_ln_w, rwkv_ln_b, w_ffn_up, ffn_conv_w, ffn_conv_b, w_ffn_down, w_ada, b_ada, norm_mix, norm_ffn, norm_final):
    batch, seq, d = x.shape
    nl = w_in.shape[0]
    m = batch * seq
    w = RWKV_WIDTH

    mod5 = _ada_mod(c, w_ada, b_ada).reshape(nl, batch, 6, 1, d)
    xf = x.reshape(m, d)
    h = _norm_mod(xf, norm_mix[0], mod5, 0, seq)
    out = None
    for l in range(nl):
        proj = _in_proj(h, _prep_w_in(w_in[l]))
        ret = _retention(proj, batch, seq)
        rwo = _rwkv(proj, batch, seq, _pad_rwkv_codes(rwkv_mu[l].reshape(1, -1)),
                    rwkv_w0[l].reshape(1, w),
                    _pad_rows(rwkv_w_up[l], RWKV_RANK_PAD).astype(BF16),
                    rwkv_a0[l].reshape(1, w),
                    _pad_rows(rwkv_a_up[l], RWKV_RANK_PAD).astype(BF16),
                    rwkv_g_up[l].astype(BF16),
                    rwkv_k_k[l].reshape(1, w), rwkv_k_a[l].reshape(1, w),
                    rwkv_r_k[l].reshape(1, w), rwkv_ln_w[l].reshape(1, w),
                    rwkv_ln_b[l].reshape(1, w))
        mo = _moba(proj, batch, seq)
        xf, h2 = _out_proj(ret, rwo, mo, w_out[l].astype(BF16), xf, mod5, l, norm_ffn[l], seq)
        act = _ffn_up(h2, w_ffn_up[l].astype(BF16), ffn_conv_w[l], ffn_conv_b[l], seq)
        if l + 1 < nl:
            xf, h = _ffn_down(act, w_ffn_down[l].astype(BF16), xf, mod5, l, norm_mix[l + 1], seq,
                              False)
        else:
            out = _ffn_down(act, w_ffn_down[l].astype(BF16), xf, mod5, l, norm_final, seq, True)
    return out.reshape(batch, seq, d)
```
